```python
import math
import jax
import jax.numpy as jnp
from jax import lax
import numpy as np

D_MODEL = 1024
BATCH = 8
SEQ = 4096
DEPTH = 1

D_MIX = D_MODEL
CONV_WIDTH = D_MIX // 2
SG_WIDTH = D_MIX - CONV_WIDTH
N_CONV_HEADS = 8
N_SG_HEADS = 8
CONV_HEAD_DIM = CONV_WIDTH // N_CONV_HEADS
SG_HEAD_DIM = SG_WIDTH // N_SG_HEADS
CONV_KERNEL = 31
SG_CHUNK = 128
PEER_HEADS = 8
PEER_TOPK = 16
N_KEYS = 128
N_EXPERTS = N_KEYS * N_KEYS
D_KEY = 256
D_HALF = D_KEY // 2
PEER_TOKEN_BLOCK = 128
PLE_DIM = 256
ALPHA = (2.0 * DEPTH) ** 0.25
BETA = (8.0 * DEPTH) ** -0.25
LN_EPS = 1e-5
D_IN = 2 * CONV_WIDTH + 2 * SG_WIDTH

kernel_name = "hymba_conformer_sgu_peer_deepnorm"


def layer_norm(x, g, b):
    xf = x.astype(jnp.float32)
    mu = jnp.mean(xf, axis=-1, keepdims=True)
    var = jnp.mean(jnp.square(xf - mu), axis=-1, keepdims=True)
    return ((xf - mu) * lax.rsqrt(var + LN_EPS)).astype(x.dtype) * g + b


def group_norm(x, g, b, groups):
    c = x.shape[-1]
    xs = x.reshape(*x.shape[:-1], groups, c // groups)
    y = layer_norm(xs, g.reshape(groups, c // groups), b.reshape(groups, c // groups))
    return y.reshape(x.shape)


def conformer_conv(a, gate, conv_w, conv_b, gn_g, gn_b):
    c = a * jax.nn.sigmoid(gate)
    rhs = conv_w[:, None, :].astype(c.dtype)
    y = lax.conv_general_dilated(
        c, rhs, window_strides=(1,), padding=[(CONV_KERNEL - 1, 0)],
        dimension_numbers=("NWC", "WIO", "NWC"),
        feature_group_count=CONV_WIDTH) + conv_b
    y = group_norm(y, gn_g, gn_b, N_CONV_HEADS)
    return jax.nn.silu(y)


def spatial_gating(u, v, ln_g, ln_b, sg_w, sg_b):
    bsz, s, _ = u.shape
    u = jax.nn.gelu(u, approximate=False)
    v = group_norm(jax.nn.gelu(v, approximate=False), ln_g, ln_b, N_SG_HEADS)
    nc = s // SG_CHUNK
    vc = v.reshape(bsz, nc, SG_CHUNK, N_SG_HEADS, SG_HEAD_DIM)
    uc = u.reshape(bsz, nc, SG_CHUNK, N_SG_HEADS, SG_HEAD_DIM)
    mask = jnp.tril(jnp.ones((SG_CHUNK, SG_CHUNK), dtype=sg_w.dtype))
    wm = sg_w * mask[None]
    f = jnp.einsum("hts,bcshd->bcthd", wm, vc) + sg_b.T[None, None, :, :, None]
    return (uc * f).reshape(bsz, s, SG_WIDTH)


def peer(h, wq, keys, u_tab, v_tab):
    bsz, s, d = h.shape
    xt = h.reshape(bsz * s // PEER_TOKEN_BLOCK, PEER_TOKEN_BLOCK, d)

    def block(xc):
        c = xc.shape[0]
        q = (xc @ wq).reshape(c, PEER_HEADS, 2, D_HALF)
        s1 = jnp.einsum("chd,hkd->chk", q[:, :, 0], keys[:, 0])
        s2 = jnp.einsum("chd,hkd->chk", q[:, :, 1], keys[:, 1])
        v1, i1 = lax.top_k(s1, PEER_TOPK)
        v2, i2 = lax.top_k(s2, PEER_TOPK)
        comb = (v1[..., :, None] + v2[..., None, :]).reshape(c, PEER_HEADS, PEER_TOPK * PEER_TOPK)
        sc, pos = lax.top_k(comb, PEER_TOPK)
        a_sel = jnp.take_along_axis(i1, pos // PEER_TOPK, axis=-1)
        b_sel = jnp.take_along_axis(i2, pos % PEER_TOPK, axis=-1)
        idx = a_sel * N_KEYS + b_sel
        g = jax.nn.softmax(sc.astype(jnp.float32), axis=-1).astype(xc.dtype)
        ug = jnp.take(u_tab, idx, axis=0)
        act = jax.nn.gelu(jnp.einsum("cd,chkd->chk", xc, ug), approximate=False)
        vg = jnp.take(v_tab, idx, axis=0)
        return jnp.einsum("chk,chkd->cd", g * act, vg)

    y = lax.map(block, xt)
    return y.reshape(bsz, s, d)


def setup_inputs(seed: int = 0) -> dict:
    key = jax.random.key(seed)
    ks = jax.random.split(key, 32)
    L = DEPTH
    f32 = jnp.float32

    def nrm(k, shape, scale):
        return jax.random.normal(k, shape, f32) * scale

    return {
        "x": nrm(ks[0], (BATCH, SEQ, D_MODEL), 1.0),
        "p": nrm(ks[1], (DEPTH, BATCH, SEQ, PLE_DIM), 1.0),
        "ln0_g": 1.0 + nrm(ks[2], (D_MODEL,), 0.02),
        "ln0_b": nrm(ks[3], (D_MODEL,), 0.02),
        "w_in": nrm(ks[4], (L, D_MODEL, D_IN), D_MODEL ** -0.5),
        "b_in": nrm(ks[5], (L, D_IN), 0.02),
        "conv_w": nrm(ks[6], (L, CONV_KERNEL, CONV_WIDTH), CONV_KERNEL ** -0.5),
        "conv_b": nrm(ks[7], (L, CONV_WIDTH), 0.02),
        "gn_g": 1.0 + nrm(ks[8], (L, CONV_WIDTH), 0.02),
        "gn_b": nrm(ks[9], (L, CONV_WIDTH), 0.02),
        "sg_ln_g": 1.0 + nrm(ks[10], (L, SG_WIDTH), 0.02),
        "sg_ln_b": nrm(ks[11], (L, SG_WIDTH), 0.02),
        "sg_w": nrm(ks[12], (L, N_SG_HEADS, SG_CHUNK, SG_CHUNK), SG_CHUNK ** -0.5),
        "sg_b": 1.0 + nrm(ks[13], (L, N_SG_HEADS, SG_CHUNK), 0.02),
        "w_o": nrm(ks[14], (L, D_MIX, D_MODEL), BETA * D_MIX ** -0.5),
        "b_o": nrm(ks[15], (L, D_MODEL), 0.02),
        "ln1_g": 1.0 + nrm(ks[16], (L, D_MODEL), 0.02),
        "ln1_b": nrm(ks[17], (L, D_MODEL), 0.02),
        "peer_wq": nrm(ks[18], (L, D_MODEL, PEER_HEADS * D_KEY), D_MODEL ** -0.5),
        "peer_keys": nrm(ks[19], (L, PEER_HEADS, 2, N_KEYS, D_HALF), D_HALF ** -0.5),
        "peer_u": nrm(ks[20], (L, N_EXPERTS, D_MODEL), D_MODEL ** -0.5),
        "peer_v": nrm(ks[21], (L, N_EXPERTS, D_MODEL), BETA * 0.5),
        "ple_wp": nrm(ks[22], (L, PLE_DIM, D_MODEL), BETA * PLE_DIM ** -0.5),
        "ple_wg": nrm(ks[23], (L, D_MODEL, D_MODEL), D_MODEL ** -0.5),
        "ple_bg": nrm(ks[24], (L, D_MODEL), 0.02),
        "ln2_g": 1.0 + nrm(ks[25], (L, D_MODEL), 0.02),
        "ln2_b": nrm(ks[26], (L, D_MODEL), 0.02),
    }


def reference(x, p, ln0_g, ln0_b, w_in, b_in, conv_w, conv_b, gn_g, gn_b,
              sg_ln_g, sg_ln_b, sg_w, sg_b, w_o, b_o, ln1_g, ln1_b,
              peer_wq, peer_keys, peer_u, peer_v, ple_wp, ple_wg, ple_bg,
              ln2_g, ln2_b):
    h = layer_norm(x, ln0_g, ln0_b)
    for i in range(DEPTH):
        z = h @ w_in[i] + b_in[i]
        ca, cg, su, sv = jnp.split(
            z, [CONV_WIDTH, 2 * CONV_WIDTH, 2 * CONV_WIDTH + SG_WIDTH], axis=-1)
        conv_out = conformer_conv(ca, cg, conv_w[i], conv_b[i], gn_g[i], gn_b[i])
        sg_out = spatial_gating(su, sv, sg_ln_g[i], sg_ln_b[i], sg_w[i], sg_b[i])
        mix = jnp.concatenate([conv_out, sg_out], axis=-1) @ w_o[i] + b_o[i]
        h = layer_norm(ALPHA * h + mix, ln1_g[i], ln1_b[i])
        ff = peer(h, peer_wq[i], peer_keys[i], peer_u[i], peer_v[i])
        ple = jax.nn.sigmoid(h @ ple_wg[i] + ple_bg[i]) * (p[i] @ ple_wp[i])
        h = layer_norm(ALPHA * h + ff + ple, ln2_g[i], ln2_b[i])
    return h
```

```python
import functools

import numpy as np
import jax
import jax.numpy as jnp
from jax import lax
from jax.experimental import pallas as pl
from jax.experimental.pallas import tpu as pltpu

D_MODEL = 1024
HALF = D_MODEL // 2
CONV_WIDTH = 512
SG_WIDTH = 512
GROUP = 64
N_SG_HEADS = 8
CONV_KERNEL = 31
SG_CHUNK = 128
PEER_HEADS = 8
PEER_TOPK = 16
N_KEYS = 128
D_KEY = 256
N_SEL = PEER_HEADS * PEER_TOPK
PLE_DIM = 256
ALPHA = 2.0 ** 0.25
LN_EPS = 1e-5
SQRT_HALF = 0.7071067811865476

CONV_HALO = 32
MIX_ROWS = 512
ROUTE_ROWS = 256
PEER_ROWS = 128
PEER_GROUP = 8
ROW_WORDS = 4
VMEM_LIMIT = 56 * 1024 * 1024

F32 = jnp.float32
BF16 = jnp.bfloat16
HIGHEST = lax.Precision.HIGHEST
NEG_INF = float("-inf")


def _layer_norm(x, g, b):
    mu = jnp.mean(x, axis=-1, keepdims=True)
    d = x - mu
    var = jnp.mean(d * d, axis=-1, keepdims=True)
    return d * lax.rsqrt(var + LN_EPS) * g + b


def _group_norm(x, avg, g, b):
    mu = jnp.dot(x, avg, precision=HIGHEST, preferred_element_type=F32)
    d = x - mu
    var = jnp.dot(d * d, avg, precision=HIGHEST, preferred_element_type=F32)
    return d * lax.rsqrt(var + LN_EPS) * g + b


def _gelu(x):
    return 0.5 * x * (1.0 + lax.erf(x * SQRT_HALF))


def _sigmoid(x):
    return 1.0 / (1.0 + jnp.exp(-x))


def _mixer_body(tiles_per_seq, x_ref, ln0g, ln0b, win, b_in, cw, cb, gng, gnb, sgg, sgb, wcat, sgbias,
                avg, wo_c, wo_s, bo, ln1g, ln1b, h1_ref, h1b_ref, cbuf):
    rows = x_ref.shape[0]

    @pl.when(pl.program_id(0) % tiles_per_seq == 0)
    def _():
        cbuf[0:CONV_HALO, :] = jnp.zeros((CONV_HALO, CONV_WIDTH), F32)

    h = _layer_norm(x_ref[...], ln0g[...], ln0b[...])
    z = jnp.dot(h.astype(BF16), win[...], preferred_element_type=F32) + b_in[...]
    ca = z[:, 0:CONV_WIDTH]
    cg = z[:, CONV_WIDTH:2 * CONV_WIDTH]
    su = z[:, 2 * CONV_WIDTH:2 * CONV_WIDTH + SG_WIDTH]
    sv = z[:, 2 * CONV_WIDTH + SG_WIDTH:]

    cbuf[CONV_HALO:CONV_HALO + rows, :] = ca * _sigmoid(cg)
    first = CONV_HALO - (CONV_KERNEL - 1)
    y = cb[...] + cw[0:1, :] * cbuf[first:first + rows, :]
    for w in range(1, CONV_KERNEL):
        y = y + cw[w:w + 1, :] * cbuf[first + w:first + w + rows, :]
    cbuf[0:CONV_HALO, :] = cbuf[rows:rows + CONV_HALO, :]
    yn = _group_norm(y, avg[...], gng[...], gnb[...])
    conv_out = yn * _sigmoid(yn)

    u = _gelu(su)
    vn = _group_norm(_gelu(sv), avg[...], sgg[...], sgb[...])
    t_i = lax.broadcasted_iota(jnp.int32, (SG_CHUNK, N_SG_HEADS * SG_CHUNK), 0)
    s_i = lax.broadcasted_iota(jnp.int32, (SG_CHUNK, N_SG_HEADS * SG_CHUNK), 1) & (SG_CHUNK - 1)
    wm = jnp.where(s_i <= t_i, wcat[...], 0.0).astype(BF16)
    lane_head = lax.broadcasted_iota(jnp.int32, (SG_CHUNK, SG_WIDTH), 1) >> 6
    parts = []
    for ci in range(rows // SG_CHUNK):
        vch = vn[ci * SG_CHUNK:(ci + 1) * SG_CHUNK, :]
        stack = jnp.concatenate(
            [jnp.where(lane_head == hd, vch, 0.0).astype(BF16) for hd in range(N_SG_HEADS)], axis=0)
        f = jnp.dot(wm, stack, preferred_element_type=F32) + sgbias[...]
        parts.append(u[ci * SG_CHUNK:(ci + 1) * SG_CHUNK, :] * f)
    sg_out = jnp.concatenate(parts, axis=0)

    mix = (jnp.dot(conv_out.astype(BF16), wo_c[...], preferred_element_type=F32)
           + jnp.dot(sg_out.astype(BF16), wo_s[...], preferred_element_type=F32) + bo[...])
    h1 = _layer_norm(ALPHA * h + mix, ln1g[...], ln1b[...])
    h1_ref[...] = h1
    h1b_ref[...] = h1.astype(BF16)


def _mixer(x2d, seq, ln0g, ln0b, win, b_in, cw, cb, gng, gnb, sgg, sgb, wcat, sgbias, avg, wo_c, wo_s, bo,
           ln1g, ln1b):
    t = x2d.shape[0]
    rows = min(MIX_ROWS, seq)
    assert seq % rows == 0 and rows % SG_CHUNK == 0
    const = lambda a: pl.BlockSpec(a.shape, lambda i: (0,) * a.ndim)
    params = (ln0g, ln0b, win, b_in, cw, cb, gng, gnb, sgg, sgb, wcat, sgbias, avg, wo_c, wo_s, bo, ln1g, ln1b)
    return pl.pallas_call(
        functools.partial(_mixer_body, seq // rows),
        grid=(t // rows,),
        in_specs=[pl.BlockSpec((rows, D_MODEL), lambda i: (i, 0))] + [const(a) for a in params],
        out_specs=[pl.BlockSpec((rows, D_MODEL), lambda i: (i, 0)),
                   pl.BlockSpec((rows, D_MODEL), lambda i: (i, 0))],
        out_shape=[jax.ShapeDtypeStruct((t, D_MODEL), F32), jax.ShapeDtypeStruct((t, D_MODEL), BF16)],
        scratch_shapes=[pltpu.VMEM((rows + CONV_HALO, CONV_WIDTH), F32)],
        compiler_params=pltpu.CompilerParams(dimension_semantics=("arbitrary",), vmem_limit_bytes=VMEM_LIMIT),
        name="mixer",
    )(x2d, *params)


def _top16_rows(s):
    n = s.shape[0]
    rid = lax.broadcasted_iota(jnp.int32, s.shape, 0)
    vals, idxs = [], []
    for _ in range(PEER_TOPK):
        m = jnp.max(s, axis=0, keepdims=True)
        ix = jnp.min(jnp.where(s == m, rid, n), axis=0, keepdims=True)
        vals.append(m)
        idxs.append(ix)
        s = jnp.where(rid == ix, NEG_INF, s)
    return jnp.concatenate(vals, axis=0), jnp.concatenate(idxs, axis=0)


def _pair_blocks():
    blocks = [("row", 0, 0), ("row", 0, 8), ("row", 1, 0), ("row", 2, 0), ("row", 3, 0),
              ("col", 0, 0), ("col", 8, 0), ("col", 0, 1), ("col", 0, 2)]
    seen, out = set(), []
    for kind, a, b in blocks:
        pairs = [(a, b + r) if kind == "row" else (a + r, b) for r in range(8)]
        good = [r for r, (i, j) in enumerate(pairs) if (i + 1) * (j + 1) <= PEER_TOPK and (i, j) not in seen]
        lo, hi = good[0], good[-1] + 1
        assert good == list(range(lo, hi))
        seen.update(pairs[r] for r in good)
        out.append((kind, a, b, lo, hi))
    assert seen == {(i, j) for i in range(16) for j in range(16) if (i + 1) * (j + 1) <= PEER_TOPK}
    return out


_PAIR_BLOCKS = _pair_blocks()


def _route_body(hb_ref, wq_ref, keys_ref, idx_ref, gate_ref):
    tok = hb_ref.shape[0]
    q = jnp.dot(hb_ref[...], wq_ref[...], preferred_element_type=F32).astype(BF16)
    nt = (((1,), (1,)), ((), ()))
    s1 = lax.dot_general(keys_ref[0, 0], q[:, 0:N_KEYS], nt, preferred_element_type=F32)
    s2 = lax.dot_general(keys_ref[0, 1], q[:, N_KEYS:], nt, preferred_element_type=F32)
    v1, i1 = _top16_rows(s1)
    v2, i2 = _top16_rows(s2)

    sub = lax.broadcasted_iota(jnp.int32, (8, tok), 0)
    comb, pos, aa, bb = [], [], [], []
    for kind, a, b, lo, hi in _PAIR_BLOCKS:
        if kind == "row":
            c = v1[a:a + 1, :] + v2[b:b + 8, :]
            ai = jnp.broadcast_to(i1[a:a + 1, :], (8, tok))
            bi = i2[b:b + 8, :]
            p = sub + (a * PEER_TOPK + b)
        else:
            c = v1[a:a + 8, :] + v2[b:b + 1, :]
            ai = i1[a:a + 8, :]
            bi = jnp.broadcast_to(i2[b:b + 1, :], (8, tok))
            p = sub * PEER_TOPK + (a * PEER_TOPK + b)
        comb.append(jnp.where((sub >= lo) & (sub < hi), c, NEG_INF))
        pos.append(p)
        aa.append(ai)
        bb.append(bi)
    comb = jnp.concatenate(comb, axis=0)
    pos = jnp.concatenate(pos, axis=0)
    aa = jnp.concatenate(aa, axis=0)
    bb = jnp.concatenate(bb, axis=0)

    big = PEER_TOPK * PEER_TOPK
    scs, ids = [], []
    for _ in range(PEER_TOPK):
        m = jnp.max(comb, axis=0, keepdims=True)
        psel = jnp.min(jnp.where(comb == m, pos, big), axis=0, keepdims=True)
        hit = pos == psel
        a_sel = jnp.max(jnp.where(hit, aa, -1), axis=0, keepdims=True)
        b_sel = jnp.max(jnp.where(hit, bb, -1), axis=0, keepdims=True)
        scs.append(m)
        ids.append(a_sel * N_KEYS + b_sel)
        comb = jnp.where(hit, NEG_INF, comb)
    sc = jnp.concatenate(scs, axis=0)
    e = jnp.exp(sc - sc[0:1, :])
    gate_ref[...] = e / jnp.sum(e, axis=0, keepdims=True)
    idx_ref[...] = jnp.concatenate(ids, axis=0) * ROW_WORDS


def _route(h1b, wq, keys):
    t = h1b.shape[0]
    rows = min(ROUTE_ROWS, t)
    return pl.pallas_call(
        _route_body,
        grid=(t // rows, PEER_HEADS),
        in_specs=[pl.BlockSpec((rows, D_MODEL), lambda i, hd: (i, 0)),
                  pl.BlockSpec((D_MODEL, D_KEY), lambda i, hd: (0, hd)),
                  pl.BlockSpec((1, 2, N_KEYS, N_KEYS), lambda i, hd: (hd, 0, 0, 0))],
        out_specs=[pl.BlockSpec((PEER_TOPK, rows), lambda i, hd: (hd, i)),
                   pl.BlockSpec((PEER_TOPK, rows), lambda i, hd: (hd, i))],
        out_shape=[jax.ShapeDtypeStruct((N_SEL, t), jnp.int32), jax.ShapeDtypeStruct((N_SEL, t), F32)],
        compiler_params=pltpu.CompilerParams(dimension_semantics=("arbitrary", "arbitrary"),
                                             vmem_limit_bytes=VMEM_LIMIT),
        name="route",
    )(h1b, wq, keys)


def _gather_rows(idx_ref, col, tab_ref, buf):
    for j in range(N_SEL):
        start = pl.multiple_of(idx_ref[j, col], ROW_WORDS)
        buf[pl.ds(ROW_WORDS * j, ROW_WORDS), :] = tab_ref[pl.ds(start, ROW_WORDS), :]
    chunks = [pltpu.bitcast(buf[pl.ds(k, N_SEL, stride=ROW_WORDS), :], BF16) for k in range(ROW_WORDS)]
    return jnp.concatenate(chunks, axis=1)


def _half_row_masks():
    row = lax.broadcasted_iota(jnp.int32, (2 * PEER_GROUP, 2 * N_SEL), 0)
    lane = lax.broadcasted_iota(jnp.int32, (2 * PEER_GROUP, 2 * N_SEL), 1)
    return row & (PEER_GROUP - 1), (lane & 1) == (row >> 3)


def _act_body(idx_ref, xlo_ref, xhi_ref, gate_ref, tab_ref, coef_ref, buf_a, buf_b, zbuf):
    tok = gate_ref.shape[1]
    owner, parity = _half_row_masks()
    nt = (((1,), (1,)), ((), ()))

    for r0 in range(0, tok, PEER_GROUP):
        xt = jnp.concatenate([xlo_ref[r0:r0 + PEER_GROUP, :], xhi_ref[r0:r0 + PEER_GROUP, :]],
                             axis=0).astype(BF16)
        zacc = jnp.zeros((2 * PEER_GROUP, 2 * N_SEL), F32)
        for i in range(PEER_GROUP):
            g = _gather_rows(idx_ref, r0 + i, tab_ref, buf_a if i % 2 == 0 else buf_b)
            y = lax.dot_general(xt, g, nt, preferred_element_type=F32)
            zacc = zacc + jnp.where(parity & (owner == i), y, 0.0)
        zbuf[r0:r0 + PEER_GROUP, :] = zacc[0:PEER_GROUP, :] + zacc[PEER_GROUP:, :]

    rr = lax.broadcasted_iota(jnp.int32, (2 * N_SEL, N_SEL), 0)
    cc = lax.broadcasted_iota(jnp.int32, (2 * N_SEL, N_SEL), 1)
    pair = ((rr >> 1) == cc).astype(F32)
    act = jnp.dot(zbuf[...], pair, precision=HIGHEST, preferred_element_type=F32)
    coef = (gate_ref[...].T * _gelu(act)).astype(BF16)
    rr2 = lax.broadcasted_iota(jnp.int32, (N_SEL, 2 * N_SEL), 0)
    cc2 = lax.broadcasted_iota(jnp.int32, (N_SEL, 2 * N_SEL), 1)
    dup = ((cc2 >> 1) == rr2).astype(BF16)
    coef_ref[...] = jnp.dot(coef, dup, preferred_element_type=F32)


def _act(idx, h1, gate, utab):
    t = gate.shape[1]
    rows = min(PEER_ROWS, t)
    return pl.pallas_call(
        _act_body,
        grid=(t // rows,),
        in_specs=[pl.BlockSpec((N_SEL, rows), lambda i: (0, i), memory_space=pltpu.SMEM,
                               pipeline_mode=pl.Buffered(1)),
                  pl.BlockSpec((rows, HALF), lambda i: (i, 0)),
                  pl.BlockSpec((rows, HALF), lambda i: (i, 1)),
                  pl.BlockSpec((N_SEL, rows), lambda i: (0, i)),
                  pl.BlockSpec(utab.shape, lambda i: (0, 0), pipeline_mode=pl.Buffered(1))],
        out_specs=pl.BlockSpec((rows, 2 * N_SEL), lambda i: (i, 0)),
        out_shape=jax.ShapeDtypeStruct((t, 2 * N_SEL), F32),
        scratch_shapes=[pltpu.VMEM((ROW_WORDS * N_SEL, 128), jnp.int32),
                        pltpu.VMEM((ROW_WORDS * N_SEL, 128), jnp.int32),
                        pltpu.VMEM((rows, 2 * N_SEL), F32)],
        compiler_params=pltpu.CompilerParams(dimension_semantics=("arbitrary",), vmem_limit_bytes=VMEM_LIMIT),
        name="peer_act",
    )(idx, h1, h1, gate, utab)


def _combine_body(idx_ref, coef_ref, h1_ref, p_ref, tab_ref, wg, bg, wp, ln2g, ln2b, out_ref, buf_a, buf_b,
                  olo, ohi):
    tok = h1_ref.shape[0]
    owner, parity = _half_row_masks()

    for r0 in range(0, tok, PEER_GROUP):
        cd = coef_ref[r0:r0 + PEER_GROUP, :]
        oacc = jnp.zeros((2 * PEER_GROUP, HALF), F32)
        for i in range(PEER_GROUP):
            g = _gather_rows(idx_ref, r0 + i, tab_ref, buf_a if i % 2 == 0 else buf_b)
            ci = jnp.broadcast_to(cd[i:i + 1, :], (2 * PEER_GROUP, 2 * N_SEL))
            lhs = jnp.where(parity & (owner == i), ci, 0.0).astype(BF16)
            oacc = oacc + jnp.dot(lhs, g, preferred_element_type=F32)
        olo[r0:r0 + PEER_GROUP, :] = oacc[0:PEER_GROUP, :]
        ohi[r0:r0 + PEER_GROUP, :] = oacc[PEER_GROUP:, :]

    ff = jnp.concatenate([olo[...], ohi[...]], axis=1)
    h1 = h1_ref[...]
    gate = _sigmoid(jnp.dot(h1.astype(BF16), wg[...], preferred_element_type=F32) + bg[...])
    ple = gate * jnp.dot(p_ref[...].astype(BF16), wp[...], preferred_element_type=F32)
    out_ref[...] = _layer_norm(ALPHA * h1 + ff + ple, ln2g[...], ln2b[...])


def _combine(idx, coef, h1, p2d, vtab, wg, bg, wp, ln2g, ln2b):
    t = h1.shape[0]
    rows = min(PEER_ROWS, t)
    const = lambda a: pl.BlockSpec(a.shape, lambda i: (0,) * a.ndim)
    return pl.pallas_call(
        _combine_body,
        grid=(t // rows,),
        in_specs=[pl.BlockSpec((N_SEL, rows), lambda i: (0, i), memory_space=pltpu.SMEM,
                               pipeline_mode=pl.Buffered(1)),
                  pl.BlockSpec((rows, 2 * N_SEL), lambda i: (i, 0)),
                  pl.BlockSpec((rows, D_MODEL), lambda i: (i, 0)),
                  pl.BlockSpec((rows, PLE_DIM), lambda i: (i, 0)),
                  pl.BlockSpec(vtab.shape, lambda i: (0, 0), pipeline_mode=pl.Buffered(1)),
                  const(wg), const(bg), const(wp), const(ln2g), const(ln2b)],
        out_specs=pl.BlockSpec((rows, D_MODEL), lambda i: (i, 0)),
        out_shape=jax.ShapeDtypeStruct((t, D_MODEL), F32),
        scratch_shapes=[pltpu.VMEM((ROW_WORDS * N_SEL, 128), jnp.int32),
                        pltpu.VMEM((ROW_WORDS * N_SEL, 128), jnp.int32),
                        pltpu.VMEM((rows, HALF), F32),
                        pltpu.VMEM((rows, HALF), F32)],
        compiler_params=pltpu.CompilerParams(dimension_semantics=("arbitrary",), vmem_limit_bytes=VMEM_LIMIT),
        name="peer_combine",
    )(idx, coef, h1, p2d, vtab, wg, bg, wp, ln2g, ln2b)


def _pack_table(tab):
    tb = tab.astype(BF16)
    pairs = jnp.stack([tb[:, :HALF], tb[:, HALF:]], axis=-1)
    return lax.bitcast_convert_type(pairs, jnp.int32).reshape(tab.shape[0] * ROW_WORDS, 128)


def kernel(x, p, ln0_g, ln0_b, w_in, b_in, conv_w, conv_b, gn_g, gn_b, sg_ln_g, sg_ln_b, sg_w, sg_b, w_o, b_o,
           ln1_g, ln1_b, peer_wq, peer_keys, peer_u, peer_v, ple_wp, ple_wg, ple_bg, ln2_g, ln2_b):
    bsz, seq, d = x.shape
    assert w_in.shape[0] == 1, "the input norm is fused into the single layer's mixer"
    t = bsz * seq
    row = lambda a: a.reshape(1, -1)
    avg = jnp.asarray(np.kron(np.eye(CONV_WIDTH // GROUP), np.full((GROUP, GROUP), 1.0 / GROUP)), F32)
    wcat = sg_w[0].transpose(1, 0, 2).reshape(SG_CHUNK, N_SG_HEADS * SG_CHUNK)
    sgbias = jnp.repeat(sg_b[0].T, GROUP, axis=1)

    h1, h1b = _mixer(x.reshape(t, d), seq, row(ln0_g), row(ln0_b), w_in[0].astype(BF16), row(b_in[0]), conv_w[0],
                     row(conv_b[0]), row(gn_g[0]), row(gn_b[0]), row(sg_ln_g[0]), row(sg_ln_b[0]), wcat, sgbias, avg,
                     w_o[0, :CONV_WIDTH].astype(BF16), w_o[0, CONV_WIDTH:].astype(BF16), row(b_o[0]),
                     row(ln1_g[0]), row(ln1_b[0]))
    idx, gate = _route(h1b, peer_wq[0].astype(BF16), peer_keys[0].astype(BF16))
    coef = _act(idx, h1, gate, _pack_table(peer_u[0]))
    out = _combine(idx, coef, h1, p[0].reshape(t, PLE_DIM), _pack_table(peer_v[0]),
                   ple_wg[0].astype(BF16), row(ple_bg[0]), ple_wp[0].astype(BF16), row(ln2_g[0]), row(ln2_b[0]))
    return out.reshape(bsz, seq, d)
```

```python
import functools

import numpy as np
import jax
import jax.numpy as jnp
from jax import lax
from jax.experimental import pallas as pl
from jax.experimental.pallas import tpu as pltpu

D_MODEL = 1024
CONV_WIDTH = 512
SG_WIDTH = 512
GROUP = 64
N_SG_HEADS = 8
CONV_KERNEL = 31
SG_CHUNK = 128
PEER_HEADS = 8
PEER_TOPK = 16
N_KEYS = 128
D_KEY = 256
N_SEL = PEER_HEADS * PEER_TOPK
PLE_DIM = 256
ALPHA = 2.0 ** 0.25
LN_EPS = 1e-5
SQRT_HALF = 0.7071067811865476

CONV_HALO = 32
MIX_ROWS = 512
ROUTE_ROWS = 256
PEER_ROWS = 128
PEER_GROUP = 8
TOKEN_CHUNKS = D_MODEL // 128
FINAL_ROWS = 512
ROW_WORDS = 4
VMEM_LIMIT = 56 * 1024 * 1024

F32 = jnp.float32
BF16 = jnp.bfloat16
HIGHEST = lax.Precision.HIGHEST
NEG_INF = float("-inf")


def _layer_norm(x, g, b):
    mu = jnp.mean(x, axis=-1, keepdims=True)
    d = x - mu
    var = jnp.mean(d * d, axis=-1, keepdims=True)
    return d * lax.rsqrt(var + LN_EPS) * g + b


def _group_norm(x, avg, g, b):
    mu = jnp.dot(x, avg, precision=HIGHEST, preferred_element_type=F32)
    d = x - mu
    var = jnp.dot(d * d, avg, precision=HIGHEST, preferred_element_type=F32)
    return d * lax.rsqrt(var + LN_EPS) * g + b


def _gelu(x):
    return 0.5 * x * (1.0 + lax.erf(x * SQRT_HALF))


def _sigmoid(x):
    return 1.0 / (1.0 + jnp.exp(-x))


def _mixer_body(tiles_per_seq, x_ref, ln0g, ln0b, win, b_in, cw, cb, gng, gnb, sgg, sgb, wcat, sgbias,
                avg, wo_c, wo_s, bo, ln1g, ln1b, h1_ref, h1b_ref, cbuf):
    rows = x_ref.shape[0]

    @pl.when(pl.program_id(0) % tiles_per_seq == 0)
    def _():
        cbuf[0:CONV_HALO, :] = jnp.zeros((CONV_HALO, CONV_WIDTH), F32)

    h = _layer_norm(x_ref[...], ln0g[...], ln0b[...])
    z = jnp.dot(h.astype(BF16), win[...], preferred_element_type=F32) + b_in[...]
    ca = z[:, 0:CONV_WIDTH]
    cg = z[:, CONV_WIDTH:2 * CONV_WIDTH]
    su = z[:, 2 * CONV_WIDTH:2 * CONV_WIDTH + SG_WIDTH]
    sv = z[:, 2 * CONV_WIDTH + SG_WIDTH:]

    cbuf[CONV_HALO:CONV_HALO + rows, :] = ca * _sigmoid(cg)
    first = CONV_HALO - (CONV_KERNEL - 1)
    y = cb[...] + cw[0:1, :] * cbuf[first:first + rows, :]
    for w in range(1, CONV_KERNEL):
        y = y + cw[w:w + 1, :] * cbuf[first + w:first + w + rows, :]
    cbuf[0:CONV_HALO, :] = cbuf[rows:rows + CONV_HALO, :]
    yn = _group_norm(y, avg[...], gng[...], gnb[...])
    conv_out = yn * _sigmoid(yn)

    u = _gelu(su)
    vn = _group_norm(_gelu(sv), avg[...], sgg[...], sgb[...])
    t_i = lax.broadcasted_iota(jnp.int32, (SG_CHUNK, N_SG_HEADS * SG_CHUNK), 0)
    s_i = lax.broadcasted_iota(jnp.int32, (SG_CHUNK, N_SG_HEADS * SG_CHUNK), 1) & (SG_CHUNK - 1)
    wm = jnp.where(s_i <= t_i, wcat[...], 0.0).astype(BF16)
    lane_head = lax.broadcasted_iota(jnp.int32, (SG_CHUNK, SG_WIDTH), 1) >> 6
    parts = []
    for ci in range(rows // SG_CHUNK):
        vch = vn[ci * SG_CHUNK:(ci + 1) * SG_CHUNK, :]
        stack = jnp.concatenate(
            [jnp.where(lane_head == hd, vch, 0.0).astype(BF16) for hd in range(N_SG_HEADS)], axis=0)
        f = jnp.dot(wm, stack, preferred_element_type=F32) + sgbias[...]
        parts.append(u[ci * SG_CHUNK:(ci + 1) * SG_CHUNK, :] * f)
    sg_out = jnp.concatenate(parts, axis=0)

    mix = (jnp.dot(conv_out.astype(BF16), wo_c[...], preferred_element_type=F32)
           + jnp.dot(sg_out.astype(BF16), wo_s[...], preferred_element_type=F32) + bo[...])
    h1 = _layer_norm(ALPHA * h + mix, ln1g[...], ln1b[...])
    h1_ref[...] = h1
    h1b_ref[...] = h1.astype(BF16)


def _mixer(x2d, seq, ln0g, ln0b, win, b_in, cw, cb, gng, gnb, sgg, sgb, wcat, sgbias, avg, wo_c, wo_s, bo,
           ln1g, ln1b):
    t = x2d.shape[0]
    rows = min(MIX_ROWS, seq)
    assert seq % rows == 0 and rows % SG_CHUNK == 0
    const = lambda a: pl.BlockSpec(a.shape, lambda i: (0,) * a.ndim)
    params = (ln0g, ln0b, win, b_in, cw, cb, gng, gnb, sgg, sgb, wcat, sgbias, avg, wo_c, wo_s, bo, ln1g, ln1b)
    return pl.pallas_call(
        functools.partial(_mixer_body, seq // rows),
        grid=(t // rows,),
        in_specs=[pl.BlockSpec((rows, D_MODEL), lambda i: (i, 0))] + [const(a) for a in params],
        out_specs=[pl.BlockSpec((rows, D_MODEL), lambda i: (i, 0)),
                   pl.BlockSpec((rows, D_MODEL), lambda i: (i, 0))],
        out_shape=[jax.ShapeDtypeStruct((t, D_MODEL), F32), jax.ShapeDtypeStruct((t, D_MODEL), BF16)],
        scratch_shapes=[pltpu.VMEM((rows + CONV_HALO, CONV_WIDTH), F32)],
        compiler_params=pltpu.CompilerParams(dimension_semantics=("arbitrary",), vmem_limit_bytes=VMEM_LIMIT),
        name="mixer",
    )(x2d, *params)


def _tile_winner(tiles):
    while len(tiles) > 1:
        nxt = []
        for k in range(0, len(tiles) - 1, 2):
            a, b = tiles[k], tiles[k + 1]
            first = a[0] >= b[0]
            nxt.append((jnp.maximum(a[0], b[0]),) + tuple(jnp.where(first, x, y) for x, y in zip(a[1:], b[1:])))
        if len(tiles) % 2:
            nxt.append(tiles[-1])
        tiles = nxt
    return tiles[0]


def _top16_rows(s):
    n, tok = s.shape
    sub = lax.broadcasted_iota(jnp.int32, (8, tok), 0)
    rank = [sub + 8 * k for k in range(n // 8)]
    val = [s[8 * k:8 * k + 8, :] for k in range(n // 8)]
    vals, idxs = [], []
    for _ in range(PEER_TOPK):
        v, r = _tile_winner(list(zip(val, rank)))
        m = jnp.max(v, axis=0, keepdims=True)
        ix = jnp.min(jnp.where(v == m, r, n), axis=0, keepdims=True)
        vals.append(m)
        idxs.append(ix)
        val = [jnp.where(rk == ix, NEG_INF, vk) for vk, rk in zip(val, rank)]
    return jnp.concatenate(vals, axis=0), jnp.concatenate(idxs, axis=0)


def _pair_tiles():
    tiles = [("row", 0, 0, 8), ("row", 0, 8, 8)]
    tiles += [("row", i, 0, min(8, PEER_TOPK // (i + 1))) for i in range(1, 8)]
    tiles += [("col", 8, 0, 8)]
    pairs = []
    for kind, a, b, n in tiles:
        pairs += [(a, b + r) if kind == "row" else (a + r, b) for r in range(n)]
    assert pairs == sorted(pairs) and len(set(pairs)) == len(pairs)
    assert set(pairs) == {(i, j) for i in range(16) for j in range(16) if (i + 1) * (j + 1) <= PEER_TOPK}
    return tiles


_PAIR_TILES = _pair_tiles()


def _route_body(hb_ref, wq_ref, keys_ref, idx_ref, gate_ref):
    tok = hb_ref.shape[0]
    q = jnp.dot(hb_ref[...], wq_ref[...], preferred_element_type=F32).astype(BF16)
    nt = (((1,), (1,)), ((), ()))
    s1 = lax.dot_general(keys_ref[0, 0], q[:, 0:N_KEYS], nt, preferred_element_type=F32)
    s2 = lax.dot_general(keys_ref[0, 1], q[:, N_KEYS:], nt, preferred_element_type=F32)
    v1, i1 = _top16_rows(s1)
    v2, i2 = _top16_rows(s2)

    sub = lax.broadcasted_iota(jnp.int32, (8, tok), 0)
    comb, rank, expert = [], [], []
    for t, (kind, a, b, n) in enumerate(_PAIR_TILES):
        if kind == "row":
            c = v1[a:a + 1, :] + v2[b:b + 8, :]
            e = i1[a:a + 1, :] * N_KEYS + i2[b:b + 8, :]
        else:
            c = v1[a:a + 8, :] + v2[b:b + 1, :]
            e = i1[a:a + 8, :] * N_KEYS + i2[b:b + 1, :]
        comb.append(c if n == 8 else jnp.where(sub < n, c, NEG_INF))
        rank.append(sub + 8 * t)
        expert.append(e)

    big = 8 * len(_PAIR_TILES)
    scs, ids = [], []
    for _ in range(PEER_TOPK):
        v, r, e = _tile_winner(list(zip(comb, rank, expert)))
        m = jnp.max(v, axis=0, keepdims=True)
        rsel = jnp.min(jnp.where(v == m, r, big), axis=0, keepdims=True)
        scs.append(m)
        ids.append(jnp.max(jnp.where(r == rsel, e, -1), axis=0, keepdims=True))
        comb = [jnp.where(rk == rsel, NEG_INF, ck) for ck, rk in zip(comb, rank)]
    sc = jnp.concatenate(scs, axis=0)
    e = jnp.exp(sc - sc[0:1, :])
    gate_ref[...] = e / jnp.sum(e, axis=0, keepdims=True)
    idx_ref[...] = jnp.concatenate(ids, axis=0) * ROW_WORDS


def _route(h1b, wq, keys):
    t = h1b.shape[0]
    rows = min(ROUTE_ROWS, t)
    return pl.pallas_call(
        _route_body,
        grid=(t // rows, PEER_HEADS),
        in_specs=[pl.BlockSpec((rows, D_MODEL), lambda i, hd: (i, 0)),
                  pl.BlockSpec((D_MODEL, D_KEY), lambda i, hd: (0, hd)),
                  pl.BlockSpec((1, 2, N_KEYS, N_KEYS), lambda i, hd: (hd, 0, 0, 0))],
        out_specs=[pl.BlockSpec((PEER_TOPK, rows), lambda i, hd: (hd, i)),
                   pl.BlockSpec((PEER_TOPK, rows), lambda i, hd: (hd, i))],
        out_shape=[jax.ShapeDtypeStruct((N_SEL, t), jnp.int32), jax.ShapeDtypeStruct((N_SEL, t), F32)],
        compiler_params=pltpu.CompilerParams(dimension_semantics=("arbitrary", "arbitrary"),
                                             vmem_limit_bytes=VMEM_LIMIT),
        name="route",
    )(h1b, wq, keys)


def _gather_matrix(idx_ref, tokn, tab_ref):
    tiles = []
    for j in range(0, N_SEL, 2):
        lo = tab_ref[pl.ds(pl.multiple_of(idx_ref[tokn * N_SEL + j], ROW_WORDS), ROW_WORDS), :]
        hi = tab_ref[pl.ds(pl.multiple_of(idx_ref[tokn * N_SEL + j + 1], ROW_WORDS), ROW_WORDS), :]
        tiles.append(pltpu.bitcast(jnp.concatenate([lo, hi], axis=0), BF16))
    return jnp.concatenate(tiles, axis=0)


def _chunk_diag():
    row = lax.broadcasted_iota(jnp.int32, (TOKEN_CHUNKS, TOKEN_CHUNKS * N_SEL), 0)
    lane = lax.broadcasted_iota(jnp.int32, (TOKEN_CHUNKS, TOKEN_CHUNKS * N_SEL), 1)
    return row, (lane & (TOKEN_CHUNKS - 1)) == row


def _idx_copy(idx_hbm, half_block, dst, sem):
    n = dst.shape[0]
    return pltpu.make_async_copy(idx_hbm.at[pl.ds(half_block * n, n)], dst, sem)


def _for_each_half(idx_hbm, smem_a, smem_b, sems, process):
    step = pl.program_id(0)
    half = smem_a.shape[0] // N_SEL

    @pl.when(step == 0)
    def _():
        _idx_copy(idx_hbm, 0, smem_a, sems.at[0]).start()

    _idx_copy(idx_hbm, 2 * step + 1, smem_b, sems.at[1]).start()
    _idx_copy(idx_hbm, 2 * step, smem_a, sems.at[0]).wait()
    process(smem_a, 0)

    @pl.when(step + 1 < pl.num_programs(0))
    def _():
        _idx_copy(idx_hbm, 2 * step + 2, smem_a, sems.at[0]).start()

    _idx_copy(idx_hbm, 2 * step + 1, smem_b, sems.at[1]).wait()
    process(smem_b, half)


def _act_body(idx_hbm, x_ref, gate_ref, tab_ref, coef_ref, smem_a, smem_b, sems, zbuf):
    row, diag = _chunk_diag()
    nt = (((1,), (1,)), ((), ()))

    def half(idx_ref, base):
        for r0 in range(0, idx_ref.shape[0] // N_SEL, PEER_GROUP):
            zt = jnp.zeros((PEER_GROUP, TOKEN_CHUNKS * N_SEL), F32)
            for i in range(PEER_GROUP):
                w = _gather_matrix(idx_ref, r0 + i, tab_ref)
                y = lax.dot_general(x_ref[base + r0 + i].astype(BF16), w, nt, preferred_element_type=F32)
                zrow = jnp.sum(jnp.where(diag, y, 0.0), axis=0, keepdims=True)
                zt = jnp.where(row == i, zrow, zt)
            zbuf[base + r0:base + r0 + PEER_GROUP, :] = zt

    _for_each_half(idx_hbm, smem_a, smem_b, sems, half)

    rr = lax.broadcasted_iota(jnp.int32, (TOKEN_CHUNKS * N_SEL, N_SEL), 0)
    cc = lax.broadcasted_iota(jnp.int32, (TOKEN_CHUNKS * N_SEL, N_SEL), 1)
    fold = ((rr >> 3) == cc).astype(F32)
    act = jnp.dot(zbuf[...], fold, precision=HIGHEST, preferred_element_type=F32)
    coef_ref[...] = gate_ref[...].T * _gelu(act)


def _peer_scratch(rows):
    half_words = (rows // 2) * N_SEL
    return [pltpu.SMEM((half_words,), jnp.int32), pltpu.SMEM((half_words,), jnp.int32),
            pltpu.SemaphoreType.DMA((2,))]


def _act(idx_flat, h1, gate, utab):
    t = gate.shape[1]
    rows = min(PEER_ROWS, t)
    return pl.pallas_call(
        _act_body,
        grid=(t // rows,),
        in_specs=[pl.BlockSpec(memory_space=pl.ANY),
                  pl.BlockSpec((rows, TOKEN_CHUNKS, 128), lambda i: (i, 0, 0)),
                  pl.BlockSpec((N_SEL, rows), lambda i: (0, i)),
                  pl.BlockSpec(utab.shape, lambda i: (0, 0), pipeline_mode=pl.Buffered(1))],
        out_specs=pl.BlockSpec((rows, N_SEL), lambda i: (i, 0)),
        out_shape=jax.ShapeDtypeStruct((t, N_SEL), F32),
        scratch_shapes=_peer_scratch(rows) + [pltpu.VMEM((rows, TOKEN_CHUNKS * N_SEL), F32)],
        compiler_params=pltpu.CompilerParams(dimension_semantics=("arbitrary",), vmem_limit_bytes=VMEM_LIMIT),
        name="peer_act",
    )(idx_flat, h1.reshape(t, TOKEN_CHUNKS, 128), gate, utab)


def _combine_body(idx_hbm, coef_ref, tab_ref, ff_ref, smem_a, smem_b, sems, crep):
    row, diag = _chunk_diag()
    rr = lax.broadcasted_iota(jnp.int32, (N_SEL, TOKEN_CHUNKS * N_SEL), 0)
    cc = lax.broadcasted_iota(jnp.int32, (N_SEL, TOKEN_CHUNKS * N_SEL), 1)
    spread = ((cc >> 3) == rr).astype(BF16)
    crep[...] = jnp.dot(coef_ref[...].astype(BF16), spread, preferred_element_type=F32)

    def half(idx_ref, base):
        for r0 in range(0, idx_ref.shape[0] // N_SEL, PEER_GROUP):
            cg = crep[base + r0:base + r0 + PEER_GROUP, :]
            for i in range(PEER_GROUP):
                w = _gather_matrix(idx_ref, r0 + i, tab_ref)
                ci = jnp.broadcast_to(cg[i:i + 1, :], diag.shape)
                lhs = jnp.where(diag, ci, 0.0).astype(BF16)
                ff_ref[base + r0 + i] = jnp.dot(lhs, w, preferred_element_type=F32)

    _for_each_half(idx_hbm, smem_a, smem_b, sems, half)


def _combine(idx_flat, coef, vtab):
    t = coef.shape[0]
    rows = min(PEER_ROWS, t)
    ff = pl.pallas_call(
        _combine_body,
        grid=(t // rows,),
        in_specs=[pl.BlockSpec(memory_space=pl.ANY),
                  pl.BlockSpec((rows, N_SEL), lambda i: (i, 0)),
                  pl.BlockSpec(vtab.shape, lambda i: (0, 0), pipeline_mode=pl.Buffered(1))],
        out_specs=pl.BlockSpec((rows, TOKEN_CHUNKS, 128), lambda i: (i, 0, 0)),
        out_shape=jax.ShapeDtypeStruct((t, TOKEN_CHUNKS, 128), F32),
        scratch_shapes=_peer_scratch(rows) + [pltpu.VMEM((rows, TOKEN_CHUNKS * N_SEL), F32)],
        compiler_params=pltpu.CompilerParams(dimension_semantics=("arbitrary",), vmem_limit_bytes=VMEM_LIMIT),
        name="peer_combine",
    )(idx_flat, coef, vtab)
    return ff.reshape(t, D_MODEL)


def _final_body(h1_ref, ff_ref, p_ref, wg, bg, wp, ln2g, ln2b, out_ref):
    h1 = h1_ref[...]
    gate = _sigmoid(jnp.dot(h1.astype(BF16), wg[...], preferred_element_type=F32) + bg[...])
    ple = gate * jnp.dot(p_ref[...].astype(BF16), wp[...], preferred_element_type=F32)
    out_ref[...] = _layer_norm(ALPHA * h1 + ff_ref[...] + ple, ln2g[...], ln2b[...])


def _final(h1, ff, p2d, wg, bg, wp, ln2g, ln2b):
    t = h1.shape[0]
    rows = min(FINAL_ROWS, t)
    const = lambda a: pl.BlockSpec(a.shape, lambda i: (0,) * a.ndim)
    return pl.pallas_call(
        _final_body,
        grid=(t // rows,),
        in_specs=[pl.BlockSpec((rows, D_MODEL), lambda i: (i, 0)),
                  pl.BlockSpec((rows, D_MODEL), lambda i: (i, 0)),
                  pl.BlockSpec((rows, PLE_DIM), lambda i: (i, 0)),
                  const(wg), const(bg), const(wp), const(ln2g), const(ln2b)],
        out_specs=pl.BlockSpec((rows, D_MODEL), lambda i: (i, 0)),
        out_shape=jax.ShapeDtypeStruct((t, D_MODEL), F32),
        compiler_params=pltpu.CompilerParams(dimension_semantics=("arbitrary",), vmem_limit_bytes=VMEM_LIMIT),
        name="final",
    )(h1, ff, p2d, wg, bg, wp, ln2g, ln2b)


def _pack_table(tab):
    n = tab.shape[0]
    pairs = tab.astype(BF16).reshape(n, ROW_WORDS, 2, 128).transpose(0, 1, 3, 2)
    return lax.bitcast_convert_type(pairs, jnp.int32).reshape(n * ROW_WORDS, 128)


def kernel(x, p, ln0_g, ln0_b, w_in, b_in, conv_w, conv_b, gn_g, gn_b, sg_ln_g, sg_ln_b, sg_w, sg_b, w_o, b_o,
           ln1_g, ln1_b, peer_wq, peer_keys, peer_u, peer_v, ple_wp, ple_wg, ple_bg, ln2_g, ln2_b):
    bsz, seq, d = x.shape
    assert w_in.shape[0] == 1, "the input norm is fused into the single layer's mixer"
    t = bsz * seq
    row = lambda a: a.reshape(1, -1)
    avg = jnp.asarray(np.kron(np.eye(CONV_WIDTH // GROUP), np.full((GROUP, GROUP), 1.0 / GROUP)), F32)
    wcat = sg_w[0].transpose(1, 0, 2).reshape(SG_CHUNK, N_SG_HEADS * SG_CHUNK)
    sgbias = jnp.repeat(sg_b[0].T, GROUP, axis=1)

    h1, h1b = _mixer(x.reshape(t, d), seq, row(ln0_g), row(ln0_b), w_in[0].astype(BF16), row(b_in[0]), conv_w[0],
                     row(conv_b[0]), row(gn_g[0]), row(gn_b[0]), row(sg_ln_g[0]), row(sg_ln_b[0]), wcat, sgbias, avg,
                     w_o[0, :CONV_WIDTH].astype(BF16), w_o[0, CONV_WIDTH:].astype(BF16), row(b_o[0]),
                     row(ln1_g[0]), row(ln1_b[0]))
    idx, gate = _route(h1b, peer_wq[0].astype(BF16), peer_keys[0].astype(BF16))
    idx_flat = idx.T.reshape(-1)
    coef = _act(idx_flat, h1, gate, _pack_table(peer_u[0]))
    ff = _combine(idx_flat, coef, _pack_table(peer_v[0]))
    out = _final(h1, ff, p[0].reshape(t, PLE_DIM), ple_wg[0].astype(BF16), row(ple_bg[0]), ple_wp[0].astype(BF16),
                 row(ln2_g[0]), row(ln2_b[0]))
    return out.reshape(bsz, seq, d)
```

```python
import functools

import numpy as np
import jax
import jax.numpy as jnp
from jax import lax
from jax.experimental import pallas as pl
from jax.experimental.pallas import tpu as pltpu

D_MODEL = 1024
CONV_WIDTH = 512
SG_WIDTH = 512
GROUP = 64
N_SG_HEADS = 8
CONV_KERNEL = 31
SG_CHUNK = 128
PEER_HEADS = 8
PEER_TOPK = 16
N_KEYS = 128
D_KEY = 256
N_SEL = PEER_HEADS * PEER_TOPK
PLE_DIM = 256
ALPHA = 2.0 ** 0.25
LN_EPS = 1e-5
SQRT_HALF = 0.7071067811865476

CONV_HALO = 32
MIX_ROWS = 512
ROUTE_ROWS = 512
PEER_ROWS = 128
PEER_GROUP = 8
TOKEN_CHUNKS = D_MODEL // 128
FINAL_ROWS = 512
PACK_ROWS = 8192
ROW_WORDS = 4
VMEM_LIMIT = 56 * 1024 * 1024

F32 = jnp.float32
BF16 = jnp.bfloat16
HIGHEST = lax.Precision.HIGHEST
NEG_INF = float("-inf")


def _layer_norm(x, g, b):
    mu = jnp.mean(x, axis=-1, keepdims=True)
    d = x - mu
    var = jnp.mean(d * d, axis=-1, keepdims=True)
    return d * lax.rsqrt(var + LN_EPS) * g + b


def _group_mean(x, avg):
    head = x.astype(BF16)
    rest = (x - head.astype(F32)).astype(BF16)
    return (jnp.dot(head, avg, preferred_element_type=F32) + jnp.dot(rest, avg, preferred_element_type=F32))


def _group_norm(x, avg, g, b):
    d = x - _group_mean(x, avg)
    var = _group_mean(d * d, avg)
    return d * lax.rsqrt(var + LN_EPS) * g + b


def _gelu(x):
    return 0.5 * x * (1.0 + lax.erf(x * SQRT_HALF))


def _sigmoid(x):
    return 1.0 / (1.0 + jnp.exp(-x))


def _mixer_body(tiles_per_seq, x_ref, ln0g, ln0b, win, b_in, cw, cb, gng, gnb, sgg, sgb, wcat, sgbias,
                avg, wo_c, wo_s, bo, ln1g, ln1b, h1_ref, h1b_ref, cbuf):
    rows = x_ref.shape[0]

    @pl.when(pl.program_id(0) % tiles_per_seq == 0)
    def _():
        cbuf[0:CONV_HALO, :] = jnp.zeros((CONV_HALO, CONV_WIDTH), F32)

    h = _layer_norm(x_ref[...], ln0g[...], ln0b[...])
    z = jnp.dot(h.astype(BF16), win[...], preferred_element_type=F32) + b_in[...]
    ca = z[:, 0:CONV_WIDTH]
    cg = z[:, CONV_WIDTH:2 * CONV_WIDTH]
    su = z[:, 2 * CONV_WIDTH:2 * CONV_WIDTH + SG_WIDTH]
    sv = z[:, 2 * CONV_WIDTH + SG_WIDTH:]

    cbuf[CONV_HALO:CONV_HALO + rows, :] = ca * _sigmoid(cg)
    first = CONV_HALO - (CONV_KERNEL - 1)
    y = jnp.broadcast_to(cb[...], (rows, CONV_WIDTH))
    for shift in range(8):
        taps = [w for w in range(CONV_KERNEL) if (first + w) % 8 == shift]
        span = max(first + w for w in taps) - shift + rows
        shifted = cbuf[shift:shift + span, :]
        for w in taps:
            off = first + w - shift
            y = y + cw[w:w + 1, :] * shifted[off:off + rows, :]
    cbuf[0:CONV_HALO, :] = cbuf[rows:rows + CONV_HALO, :]
    yn = _group_norm(y, avg[...], gng[...], gnb[...])
    conv_out = yn * _sigmoid(yn)

    u = _gelu(su)
    vn = _group_norm(_gelu(sv), avg[...], sgg[...], sgb[...])
    t_i = lax.broadcasted_iota(jnp.int32, (SG_CHUNK, N_SG_HEADS * SG_CHUNK), 0)
    s_i = lax.broadcasted_iota(jnp.int32, (SG_CHUNK, N_SG_HEADS * SG_CHUNK), 1) & (SG_CHUNK - 1)
    wm = jnp.where(s_i <= t_i, wcat[...], 0.0).astype(BF16)
    lane_head = lax.broadcasted_iota(jnp.int32, (SG_CHUNK, SG_WIDTH), 1) >> 6
    parts = []
    for ci in range(rows // SG_CHUNK):
        vch = vn[ci * SG_CHUNK:(ci + 1) * SG_CHUNK, :]
        stack = jnp.concatenate(
            [jnp.where(lane_head == hd, vch, 0.0).astype(BF16) for hd in range(N_SG_HEADS)], axis=0)
        f = jnp.dot(wm, stack, preferred_element_type=F32) + sgbias[...]
        parts.append(u[ci * SG_CHUNK:(ci + 1) * SG_CHUNK, :] * f)
    sg_out = jnp.concatenate(parts, axis=0)

    mix = (jnp.dot(conv_out.astype(BF16), wo_c[...], preferred_element_type=F32)
           + jnp.dot(sg_out.astype(BF16), wo_s[...], preferred_element_type=F32) + bo[...])
    h1 = _layer_norm(ALPHA * h + mix, ln1g[...], ln1b[...])
    h1_ref[...] = h1
    h1b_ref[...] = h1.astype(BF16)


def _mixer(x2d, seq, ln0g, ln0b, win, b_in, cw, cb, gng, gnb, sgg, sgb, wcat, sgbias, avg, wo_c, wo_s, bo,
           ln1g, ln1b):
    t = x2d.shape[0]
    rows = min(MIX_ROWS, seq)
    assert seq % rows == 0 and rows % SG_CHUNK == 0
    const = lambda a: pl.BlockSpec(a.shape, lambda i: (0,) * a.ndim)
    params = (ln0g, ln0b, win, b_in, cw, cb, gng, gnb, sgg, sgb, wcat, sgbias, avg, wo_c, wo_s, bo, ln1g, ln1b)
    return pl.pallas_call(
        functools.partial(_mixer_body, seq // rows),
        grid=(t // rows,),
        in_specs=[pl.BlockSpec((rows, D_MODEL), lambda i: (i, 0))] + [const(a) for a in params],
        out_specs=[pl.BlockSpec((rows, D_MODEL), lambda i: (i, 0)),
                   pl.BlockSpec((rows, D_MODEL), lambda i: (i, 0))],
        out_shape=[jax.ShapeDtypeStruct((t, D_MODEL), F32), jax.ShapeDtypeStruct((t, D_MODEL), BF16)],
        scratch_shapes=[pltpu.VMEM((rows + CONV_HALO, CONV_WIDTH), F32)],
        compiler_params=pltpu.CompilerParams(dimension_semantics=("arbitrary",), vmem_limit_bytes=VMEM_LIMIT),
        name="mixer",
    )(x2d, *params)


def _tile_winner(tiles):
    while len(tiles) > 1:
        nxt = []
        for k in range(0, len(tiles) - 1, 2):
            a, b = tiles[k], tiles[k + 1]
            first = a[0] >= b[0]
            nxt.append((jnp.maximum(a[0], b[0]),) + tuple(jnp.where(first, x, y) for x, y in zip(a[1:], b[1:])))
        if len(tiles) % 2:
            nxt.append(tiles[-1])
        tiles = nxt
    return tiles[0]


def _top16_rows(s):
    n, tok = s.shape
    sub = lax.broadcasted_iota(jnp.int32, (8, tok), 0)
    rank = [sub + 8 * k for k in range(n // 8)]
    val = [s[8 * k:8 * k + 8, :] for k in range(n // 8)]
    vals, idxs = [], []
    for _ in range(PEER_TOPK):
        v, r = _tile_winner(list(zip(val, rank)))
        m = jnp.max(v, axis=0, keepdims=True)
        ix = jnp.min(jnp.where(v == m, r, n), axis=0, keepdims=True)
        vals.append(m)
        idxs.append(ix)
        val = [jnp.where(rk == ix, NEG_INF, vk) for vk, rk in zip(val, rank)]
    return jnp.concatenate(vals, axis=0), jnp.concatenate(idxs, axis=0)


def _pair_tiles():
    tiles = [("row", 0, 0, 8), ("row", 0, 8, 8)]
    tiles += [("row", i, 0, min(8, PEER_TOPK // (i + 1))) for i in range(1, 8)]
    tiles += [("col", 8, 0, 8)]
    pairs = []
    for kind, a, b, n in tiles:
        pairs += [(a, b + r) if kind == "row" else (a + r, b) for r in range(n)]
    assert pairs == sorted(pairs) and len(set(pairs)) == len(pairs)
    assert set(pairs) == {(i, j) for i in range(16) for j in range(16) if (i + 1) * (j + 1) <= PEER_TOPK}
    return tiles


_PAIR_TILES = _pair_tiles()


def _route_body(hb_ref, wq_ref, keys_ref, idx_ref, gate_ref):
    tok = hb_ref.shape[0]
    q = jnp.dot(hb_ref[...], wq_ref[...], preferred_element_type=F32).astype(BF16)
    nt = (((1,), (1,)), ((), ()))
    s1 = lax.dot_general(keys_ref[0, 0], q[:, 0:N_KEYS], nt, preferred_element_type=F32)
    s2 = lax.dot_general(keys_ref[0, 1], q[:, N_KEYS:], nt, preferred_element_type=F32)
    v1, i1 = _top16_rows(s1)
    v2, i2 = _top16_rows(s2)

    sub = lax.broadcasted_iota(jnp.int32, (8, tok), 0)
    comb, rank, expert = [], [], []
    for t, (kind, a, b, n) in enumerate(_PAIR_TILES):
        if kind == "row":
            c = v1[a:a + 1, :] + v2[b:b + 8, :]
            e = i1[a:a + 1, :] * N_KEYS + i2[b:b + 8, :]
        else:
            c = v1[a:a + 8, :] + v2[b:b + 1, :]
            e = i1[a:a + 8, :] * N_KEYS + i2[b:b + 1, :]
        comb.append(c if n == 8 else jnp.where(sub < n, c, NEG_INF))
        rank.append(sub + 8 * t)
        expert.append(e)

    big = 8 * len(_PAIR_TILES)
    scs, ids = [], []
    for _ in range(PEER_TOPK):
        v, r, e = _tile_winner(list(zip(comb, rank, expert)))
        m = jnp.max(v, axis=0, keepdims=True)
        rsel = jnp.min(jnp.where(v == m, r, big), axis=0, keepdims=True)
        scs.append(m)
        ids.append(jnp.max(jnp.where(r == rsel, e, -1), axis=0, keepdims=True))
        comb = [jnp.where(rk == rsel, NEG_INF, ck) for ck, rk in zip(comb, rank)]
    sc = jnp.concatenate(scs, axis=0)
    e = jnp.exp(sc - sc[0:1, :])
    gate_ref[...] = e / jnp.sum(e, axis=0, keepdims=True)
    idx_ref[...] = jnp.concatenate(ids, axis=0) * ROW_WORDS


def _route(h1b, wq, keys):
    t = h1b.shape[0]
    rows = min(ROUTE_ROWS, t)
    return pl.pallas_call(
        _route_body,
        grid=(t // rows, PEER_HEADS),
        in_specs=[pl.BlockSpec((rows, D_MODEL), lambda i, hd: (i, 0)),
                  pl.BlockSpec((D_MODEL, D_KEY), lambda i, hd: (0, hd)),
                  pl.BlockSpec((1, 2, N_KEYS, N_KEYS), lambda i, hd: (hd, 0, 0, 0))],
        out_specs=[pl.BlockSpec((PEER_TOPK, rows), lambda i, hd: (hd, i)),
                   pl.BlockSpec((PEER_TOPK, rows), lambda i, hd: (hd, i))],
        out_shape=[jax.ShapeDtypeStruct((N_SEL, t), jnp.int32), jax.ShapeDtypeStruct((N_SEL, t), F32)],
        compiler_params=pltpu.CompilerParams(dimension_semantics=("arbitrary", "arbitrary"),
                                             vmem_limit_bytes=VMEM_LIMIT),
        name="route",
    )(h1b, wq, keys)


def _gather_matrix(idx_ref, tokn, tab_ref):
    tiles = []
    for j in range(0, N_SEL, 2):
        lo = tab_ref[pl.ds(pl.multiple_of(idx_ref[tokn * N_SEL + j], ROW_WORDS), ROW_WORDS), :]
        hi = tab_ref[pl.ds(pl.multiple_of(idx_ref[tokn * N_SEL + j + 1], ROW_WORDS), ROW_WORDS), :]
        tiles.append(pltpu.bitcast(jnp.concatenate([lo, hi], axis=0), BF16))
    return jnp.concatenate(tiles, axis=0)


def _chunk_diag():
    row = lax.broadcasted_iota(jnp.int32, (TOKEN_CHUNKS, TOKEN_CHUNKS * N_SEL), 0)
    lane = lax.broadcasted_iota(jnp.int32, (TOKEN_CHUNKS, TOKEN_CHUNKS * N_SEL), 1)
    return row, (lane & (TOKEN_CHUNKS - 1)) == row


def _idx_copy(idx_hbm, half_block, dst, sem):
    n = dst.shape[0]
    return pltpu.make_async_copy(idx_hbm.at[pl.ds(half_block * n, n)], dst, sem)


def _for_each_half(idx_hbm, smem_a, smem_b, sems, process):
    step = pl.program_id(0)
    half = smem_a.shape[0] // N_SEL

    @pl.when(step == 0)
    def _():
        _idx_copy(idx_hbm, 0, smem_a, sems.at[0]).start()

    _idx_copy(idx_hbm, 2 * step + 1, smem_b, sems.at[1]).start()
    _idx_copy(idx_hbm, 2 * step, smem_a, sems.at[0]).wait()
    process(smem_a, 0)

    @pl.when(step + 1 < pl.num_programs(0))
    def _():
        _idx_copy(idx_hbm, 2 * step + 2, smem_a, sems.at[0]).start()

    _idx_copy(idx_hbm, 2 * step + 1, smem_b, sems.at[1]).wait()
    process(smem_b, half)


def _act_body(idx_hbm, x_ref, gate_ref, tab_ref, coef_ref, smem_a, smem_b, sems, zbuf):
    row, diag = _chunk_diag()
    nt = (((1,), (1,)), ((), ()))

    def half(idx_ref, base):
        for r0 in range(0, idx_ref.shape[0] // N_SEL, PEER_GROUP):
            zt = jnp.zeros((PEER_GROUP, TOKEN_CHUNKS * N_SEL), F32)
            for i in range(PEER_GROUP):
                w = _gather_matrix(idx_ref, r0 + i, tab_ref)
                y = lax.dot_general(x_ref[base + r0 + i].astype(BF16), w, nt, preferred_element_type=F32)
                zrow = jnp.sum(jnp.where(diag, y, 0.0), axis=0, keepdims=True)
                zt = jnp.where(row == i, zrow, zt)
            zbuf[base + r0:base + r0 + PEER_GROUP, :] = zt

    _for_each_half(idx_hbm, smem_a, smem_b, sems, half)

    rr = lax.broadcasted_iota(jnp.int32, (TOKEN_CHUNKS * N_SEL, N_SEL), 0)
    cc = lax.broadcasted_iota(jnp.int32, (TOKEN_CHUNKS * N_SEL, N_SEL), 1)
    fold = ((rr >> 3) == cc).astype(F32)
    act = jnp.dot(zbuf[...], fold, precision=HIGHEST, preferred_element_type=F32)
    coef_ref[...] = gate_ref[...].T * _gelu(act)


def _peer_scratch(rows):
    half_words = (rows // 2) * N_SEL
    return [pltpu.SMEM((half_words,), jnp.int32), pltpu.SMEM((half_words,), jnp.int32),
            pltpu.SemaphoreType.DMA((2,))]


def _act(idx_flat, h1, gate, utab):
    t = gate.shape[1]
    rows = min(PEER_ROWS, t)
    return pl.pallas_call(
        _act_body,
        grid=(t // rows,),
        in_specs=[pl.BlockSpec(memory_space=pl.ANY),
                  pl.BlockSpec((rows, TOKEN_CHUNKS, 128), lambda i: (i, 0, 0)),
                  pl.BlockSpec((N_SEL, rows), lambda i: (0, i)),
                  pl.BlockSpec(utab.shape, lambda i: (0, 0), pipeline_mode=pl.Buffered(1))],
        out_specs=pl.BlockSpec((rows, N_SEL), lambda i: (i, 0)),
        out_shape=jax.ShapeDtypeStruct((t, N_SEL), F32),
        scratch_shapes=_peer_scratch(rows) + [pltpu.VMEM((rows, TOKEN_CHUNKS * N_SEL), F32)],
        compiler_params=pltpu.CompilerParams(dimension_semantics=("arbitrary",), vmem_limit_bytes=VMEM_LIMIT),
        name="peer_act",
    )(idx_flat, h1.reshape(t, TOKEN_CHUNKS, 128), gate, utab)


def _combine_body(idx_hbm, coef_ref, tab_ref, ff_ref, smem_a, smem_b, sems, crep):
    row, diag = _chunk_diag()
    rr = lax.broadcasted_iota(jnp.int32, (N_SEL, TOKEN_CHUNKS * N_SEL), 0)
    cc = lax.broadcasted_iota(jnp.int32, (N_SEL, TOKEN_CHUNKS * N_SEL), 1)
    spread = ((cc >> 3) == rr).astype(BF16)
    crep[...] = jnp.dot(coef_ref[...].astype(BF16), spread, preferred_element_type=F32)

    def half(idx_ref, base):
        for r0 in range(0, idx_ref.shape[0] // N_SEL, PEER_GROUP):
            cg = crep[base + r0:base + r0 + PEER_GROUP, :]
            for i in range(PEER_GROUP):
                w = _gather_matrix(idx_ref, r0 + i, tab_ref)
                ci = jnp.broadcast_to(cg[i:i + 1, :], diag.shape)
                lhs = jnp.where(diag, ci, 0.0).astype(BF16)
                ff_ref[base + r0 + i] = jnp.dot(lhs, w, preferred_element_type=F32)

    _for_each_half(idx_hbm, smem_a, smem_b, sems, half)


def _combine(idx_flat, coef, vtab):
    t = coef.shape[0]
    rows = min(PEER_ROWS, t)
    ff = pl.pallas_call(
        _combine_body,
        grid=(t // rows,),
        in_specs=[pl.BlockSpec(memory_space=pl.ANY),
                  pl.BlockSpec((rows, N_SEL), lambda i: (i, 0)),
                  pl.BlockSpec(vtab.shape, lambda i: (0, 0), pipeline_mode=pl.Buffered(1))],
        out_specs=pl.BlockSpec((rows, TOKEN_CHUNKS, 128), lambda i: (i, 0, 0)),
        out_shape=jax.ShapeDtypeStruct((t, TOKEN_CHUNKS, 128), F32),
        scratch_shapes=_peer_scratch(rows) + [pltpu.VMEM((rows, TOKEN_CHUNKS * N_SEL), F32)],
        compiler_params=pltpu.CompilerParams(dimension_semantics=("arbitrary",), vmem_limit_bytes=VMEM_LIMIT),
        name="peer_combine",
    )(idx_flat, coef, vtab)
    return ff.reshape(t, D_MODEL)


def _final_body(h1_ref, ff_ref, p_ref, wg, bg, wp, ln2g, ln2b, out_ref):
    h1 = h1_ref[...]
    gate = _sigmoid(jnp.dot(h1.astype(BF16), wg[...], preferred_element_type=F32) + bg[...])
    ple = gate * jnp.dot(p_ref[...].astype(BF16), wp[...], preferred_element_type=F32)
    out_ref[...] = _layer_norm(ALPHA * h1 + ff_ref[...] + ple, ln2g[...], ln2b[...])


def _final(h1, ff, p2d, wg, bg, wp, ln2g, ln2b):
    t = h1.shape[0]
    rows = min(FINAL_ROWS, t)
    const = lambda a: pl.BlockSpec(a.shape, lambda i: (0,) * a.ndim)
    return pl.pallas_call(
        _final_body,
        grid=(t // rows,),
        in_specs=[pl.BlockSpec((rows, D_MODEL), lambda i: (i, 0)),
                  pl.BlockSpec((rows, D_MODEL), lambda i: (i, 0)),
                  pl.BlockSpec((rows, PLE_DIM), lambda i: (i, 0)),
                  const(wg), const(bg), const(wp), const(ln2g), const(ln2b)],
        out_specs=pl.BlockSpec((rows, D_MODEL), lambda i: (i, 0)),
        out_shape=jax.ShapeDtypeStruct((t, D_MODEL), F32),
        compiler_params=pltpu.CompilerParams(dimension_semantics=("arbitrary",), vmem_limit_bytes=VMEM_LIMIT),
        name="final",
    )(h1, ff, p2d, wg, bg, wp, ln2g, ln2b)


def _pack_body(tab_ref, out_ref):
    out_ref[...] = pltpu.bitcast(tab_ref[...].astype(BF16), jnp.int32)


def _pack_table(tab):
    n = tab.shape[0]
    view_rows = n * TOKEN_CHUNKS
    rows = min(PACK_ROWS, view_rows)
    return pl.pallas_call(
        _pack_body,
        grid=(view_rows // rows,),
        in_specs=[pl.BlockSpec((rows, 128), lambda i: (i, 0))],
        out_specs=pl.BlockSpec((rows // 2, 128), lambda i: (i, 0)),
        out_shape=jax.ShapeDtypeStruct((view_rows // 2, 128), jnp.int32),
        compiler_params=pltpu.CompilerParams(dimension_semantics=("arbitrary",), vmem_limit_bytes=VMEM_LIMIT),
        name="pack_table",
    )(tab.reshape(view_rows, 128))


def kernel(x, p, ln0_g, ln0_b, w_in, b_in, conv_w, conv_b, gn_g, gn_b, sg_ln_g, sg_ln_b, sg_w, sg_b, w_o, b_o,
           ln1_g, ln1_b, peer_wq, peer_keys, peer_u, peer_v, ple_wp, ple_wg, ple_bg, ln2_g, ln2_b):
    bsz, seq, d = x.shape
    assert w_in.shape[0] == 1, "the input norm is fused into the single layer's mixer"
    t = bsz * seq
    row = lambda a: a.reshape(1, -1)
    avg = jnp.asarray(np.kron(np.eye(CONV_WIDTH // GROUP), np.full((GROUP, GROUP), 1.0 / GROUP)), BF16)
    wcat = sg_w[0].transpose(1, 0, 2).reshape(SG_CHUNK, N_SG_HEADS * SG_CHUNK)
    sgbias = jnp.repeat(sg_b[0].T, GROUP, axis=1)

    h1, h1b = _mixer(x.reshape(t, d), seq, row(ln0_g), row(ln0_b), w_in[0].astype(BF16), row(b_in[0]), conv_w[0],
                     row(conv_b[0]), row(gn_g[0]), row(gn_b[0]), row(sg_ln_g[0]), row(sg_ln_b[0]), wcat, sgbias, avg,
                     w_o[0, :CONV_WIDTH].astype(BF16), w_o[0, CONV_WIDTH:].astype(BF16), row(b_o[0]),
                     row(ln1_g[0]), row(ln1_b[0]))
    idx, gate = _route(h1b, peer_wq[0].astype(BF16), peer_keys[0].astype(BF16))
    idx_flat = idx.T.reshape(-1)
    coef = _act(idx_flat, h1, gate, _pack_table(peer_u[0]))
    ff = _combine(idx_flat, coef, _pack_table(peer_v[0]))
    out = _final(h1, ff, p[0].reshape(t, PLE_DIM), ple_wg[0].astype(BF16), row(ple_bg[0]), ple_wp[0].astype(BF16),
                 row(ln2_g[0]), row(ln2_b[0]))
    return out.reshape(bsz, seq, d)
```

```python
import functools

import numpy as np
import jax
import jax.numpy as jnp
from jax import lax
from jax.experimental import pallas as pl
from jax.experimental.pallas import tpu as pltpu

D_MODEL = 1024
CONV_WIDTH = 512
SG_WIDTH = 512
GROUP = 64
N_SG_HEADS = 8
CONV_KERNEL = 31
SG_CHUNK = 128
PEER_HEADS = 8
PEER_TOPK = 16
N_KEYS = 128
D_KEY = 256
N_SEL = PEER_HEADS * PEER_TOPK
PLE_DIM = 256
ALPHA = 2.0 ** 0.25
LN_EPS = 1e-5
SQRT_HALF = 0.7071067811865476

CONV_HALO = 32
MIX_ROWS = 512
ROUTE_ROWS = 1024
PEER_ROWS = 128
PEER_GROUP = 8
TOKEN_CHUNKS = D_MODEL // 128
FINAL_ROWS = 512
PACK_ROWS = 8192
ROW_WORDS = 4
VMEM_LIMIT = 56 * 1024 * 1024

F32 = jnp.float32
BF16 = jnp.bfloat16
HIGHEST = lax.Precision.HIGHEST
NEG_INF = float("-inf")


def _layer_norm(x, g, b):
    mu = jnp.mean(x, axis=-1, keepdims=True)
    d = x - mu
    var = jnp.mean(d * d, axis=-1, keepdims=True)
    return d * lax.rsqrt(var + LN_EPS) * g + b


def _group_mean(x, avg):
    head = x.astype(BF16)
    rest = (x - head.astype(F32)).astype(BF16)
    return (jnp.dot(head, avg, preferred_element_type=F32) + jnp.dot(rest, avg, preferred_element_type=F32))


def _group_norm(x, avg, g, b):
    d = x - _group_mean(x, avg)
    var = _group_mean(d * d, avg)
    return d * lax.rsqrt(var + LN_EPS) * g + b


def _gelu(x):
    return 0.5 * x * (1.0 + lax.erf(x * SQRT_HALF))


def _sigmoid(x):
    return 1.0 / (1.0 + jnp.exp(-x))


def _mixer_body(tiles_per_seq, x_ref, ln0g, ln0b, win, b_in, cw, cb, gng, gnb, sgg, sgb, wcat, sgbias,
                avg, wo_c, wo_s, bo, ln1g, ln1b, h1_ref, h1b_ref, cbuf):
    rows = x_ref.shape[0]

    @pl.when(pl.program_id(0) % tiles_per_seq == 0)
    def _():
        cbuf[0:CONV_HALO, :] = jnp.zeros((CONV_HALO, CONV_WIDTH), F32)

    h = _layer_norm(x_ref[...], ln0g[...], ln0b[...])
    z = jnp.dot(h.astype(BF16), win[...], preferred_element_type=F32) + b_in[...]
    ca = z[:, 0:CONV_WIDTH]
    cg = z[:, CONV_WIDTH:2 * CONV_WIDTH]
    su = z[:, 2 * CONV_WIDTH:2 * CONV_WIDTH + SG_WIDTH]
    sv = z[:, 2 * CONV_WIDTH + SG_WIDTH:]

    cbuf[CONV_HALO:CONV_HALO + rows, :] = ca * _sigmoid(cg)
    first = CONV_HALO - (CONV_KERNEL - 1)
    y = jnp.broadcast_to(cb[...], (rows, CONV_WIDTH))
    for shift in range(8):
        taps = [w for w in range(CONV_KERNEL) if (first + w) % 8 == shift]
        span = max(first + w for w in taps) - shift + rows
        shifted = cbuf[shift:shift + span, :]
        for w in taps:
            off = first + w - shift
            y = y + cw[w:w + 1, :] * shifted[off:off + rows, :]
    cbuf[0:CONV_HALO, :] = cbuf[rows:rows + CONV_HALO, :]
    yn = _group_norm(y, avg[...], gng[...], gnb[...])
    conv_out = yn * _sigmoid(yn)

    u = _gelu(su)
    vn = _group_norm(_gelu(sv), avg[...], sgg[...], sgb[...])
    t_i = lax.broadcasted_iota(jnp.int32, (SG_CHUNK, N_SG_HEADS * SG_CHUNK), 0)
    s_i = lax.broadcasted_iota(jnp.int32, (SG_CHUNK, N_SG_HEADS * SG_CHUNK), 1) & (SG_CHUNK - 1)
    wm = jnp.where(s_i <= t_i, wcat[...], 0.0).astype(BF16)
    lane_head = lax.broadcasted_iota(jnp.int32, (SG_CHUNK, SG_WIDTH), 1) >> 6
    parts = []
    for ci in range(rows // SG_CHUNK):
        vch = vn[ci * SG_CHUNK:(ci + 1) * SG_CHUNK, :]
        stack = jnp.concatenate(
            [jnp.where(lane_head == hd, vch, 0.0).astype(BF16) for hd in range(N_SG_HEADS)], axis=0)
        f = jnp.dot(wm, stack, preferred_element_type=F32) + sgbias[...]
        parts.append(u[ci * SG_CHUNK:(ci + 1) * SG_CHUNK, :] * f)
    sg_out = jnp.concatenate(parts, axis=0)

    mix = (jnp.dot(conv_out.astype(BF16), wo_c[...], preferred_element_type=F32)
           + jnp.dot(sg_out.astype(BF16), wo_s[...], preferred_element_type=F32) + bo[...])
    h1 = _layer_norm(ALPHA * h + mix, ln1g[...], ln1b[...])
    h1_ref[...] = h1
    h1b_ref[...] = h1.astype(BF16)


def _mixer(x2d, seq, ln0g, ln0b, win, b_in, cw, cb, gng, gnb, sgg, sgb, wcat, sgbias, avg, wo_c, wo_s, bo,
           ln1g, ln1b):
    t = x2d.shape[0]
    rows = min(MIX_ROWS, seq)
    assert seq % rows == 0 and rows % SG_CHUNK == 0
    const = lambda a: pl.BlockSpec(a.shape, lambda i: (0,) * a.ndim)
    params = (ln0g, ln0b, win, b_in, cw, cb, gng, gnb, sgg, sgb, wcat, sgbias, avg, wo_c, wo_s, bo, ln1g, ln1b)
    return pl.pallas_call(
        functools.partial(_mixer_body, seq // rows),
        grid=(t // rows,),
        in_specs=[pl.BlockSpec((rows, D_MODEL), lambda i: (i, 0))] + [const(a) for a in params],
        out_specs=[pl.BlockSpec((rows, D_MODEL), lambda i: (i, 0)),
                   pl.BlockSpec((rows, D_MODEL), lambda i: (i, 0))],
        out_shape=[jax.ShapeDtypeStruct((t, D_MODEL), F32), jax.ShapeDtypeStruct((t, D_MODEL), BF16)],
        scratch_shapes=[pltpu.VMEM((rows + CONV_HALO, CONV_WIDTH), F32)],
        compiler_params=pltpu.CompilerParams(dimension_semantics=("arbitrary",), vmem_limit_bytes=VMEM_LIMIT),
        name="mixer",
    )(x2d, *params)


def _tile_winner(tiles):
    while len(tiles) > 1:
        nxt = []
        for k in range(0, len(tiles) - 1, 2):
            a, b = tiles[k], tiles[k + 1]
            first = a[0] >= b[0]
            nxt.append((jnp.maximum(a[0], b[0]),) + tuple(jnp.where(first, x, y) for x, y in zip(a[1:], b[1:])))
        if len(tiles) % 2:
            nxt.append(tiles[-1])
        tiles = nxt
    return tiles[0]


def _top16_rows(s):
    n, tok = s.shape
    sub = lax.broadcasted_iota(jnp.int32, (8, tok), 0)
    rank = [sub + 8 * k for k in range(n // 8)]
    val = [s[8 * k:8 * k + 8, :] for k in range(n // 8)]
    vals, idxs = [], []
    for _ in range(PEER_TOPK):
        v, r = _tile_winner(list(zip(val, rank)))
        m = jnp.max(v, axis=0, keepdims=True)
        ix = jnp.min(jnp.where(v == m, r, n), axis=0, keepdims=True)
        vals.append(m)
        idxs.append(ix)
        val = [jnp.where(rk == ix, NEG_INF, vk) for vk, rk in zip(val, rank)]
    return jnp.concatenate(vals, axis=0), jnp.concatenate(idxs, axis=0)


def _pair_tiles():
    tiles = [("row", 0, 0, 8), ("row", 0, 8, 8)]
    tiles += [("row", i, 0, min(8, PEER_TOPK // (i + 1))) for i in range(1, 8)]
    tiles += [("col", 8, 0, 8)]
    pairs = []
    for kind, a, b, n in tiles:
        pairs += [(a, b + r) if kind == "row" else (a + r, b) for r in range(n)]
    assert pairs == sorted(pairs) and len(set(pairs)) == len(pairs)
    assert set(pairs) == {(i, j) for i in range(16) for j in range(16) if (i + 1) * (j + 1) <= PEER_TOPK}
    return tiles


_PAIR_TILES = _pair_tiles()


def _route_body(hb_ref, wq_ref, keys_ref, idx_ref, gate_ref):
    tok = hb_ref.shape[0]
    q = jnp.dot(hb_ref[...], wq_ref[...], preferred_element_type=F32).astype(BF16)
    nt = (((1,), (1,)), ((), ()))
    s1 = lax.dot_general(keys_ref[0, 0], q[:, 0:N_KEYS], nt, preferred_element_type=F32)
    s2 = lax.dot_general(keys_ref[0, 1], q[:, N_KEYS:], nt, preferred_element_type=F32)
    v1, i1 = _top16_rows(s1)
    v2, i2 = _top16_rows(s2)

    sub = lax.broadcasted_iota(jnp.int32, (8, tok), 0)
    comb, rank, expert = [], [], []
    for t, (kind, a, b, n) in enumerate(_PAIR_TILES):
        if kind == "row":
            c = v1[a:a + 1, :] + v2[b:b + 8, :]
            e = i1[a:a + 1, :] * N_KEYS + i2[b:b + 8, :]
        else:
            c = v1[a:a + 8, :] + v2[b:b + 1, :]
            e = i1[a:a + 8, :] * N_KEYS + i2[b:b + 1, :]
        comb.append(c if n == 8 else jnp.where(sub < n, c, NEG_INF))
        rank.append(sub + 8 * t)
        expert.append(e)

    big = 8 * len(_PAIR_TILES)
    scs, ids = [], []
    for _ in range(PEER_TOPK):
        v, r, e = _tile_winner(list(zip(comb, rank, expert)))
        m = jnp.max(v, axis=0, keepdims=True)
        rsel = jnp.min(jnp.where(v == m, r, big), axis=0, keepdims=True)
        scs.append(m)
        ids.append(jnp.max(jnp.where(r == rsel, e, -1), axis=0, keepdims=True))
        comb = [jnp.where(rk == rsel, NEG_INF, ck) for ck, rk in zip(comb, rank)]
    sc = jnp.concatenate(scs, axis=0)
    e = jnp.exp(sc - sc[0:1, :])
    gate_ref[...] = e / jnp.sum(e, axis=0, keepdims=True)
    idx_ref[...] = jnp.concatenate(ids, axis=0) * ROW_WORDS


def _route(h1b, wq, keys):
    t = h1b.shape[0]
    rows = min(ROUTE_ROWS, t)
    return pl.pallas_call(
        _route_body,
        grid=(t // rows, PEER_HEADS),
        in_specs=[pl.BlockSpec((rows, D_MODEL), lambda i, hd: (i, 0)),
                  pl.BlockSpec((D_MODEL, D_KEY), lambda i, hd: (0, hd)),
                  pl.BlockSpec((1, 2, N_KEYS, N_KEYS), lambda i, hd: (hd, 0, 0, 0))],
        out_specs=[pl.BlockSpec((PEER_TOPK, rows), lambda i, hd: (hd, i)),
                   pl.BlockSpec((PEER_TOPK, rows), lambda i, hd: (hd, i))],
        out_shape=[jax.ShapeDtypeStruct((N_SEL, t), jnp.int32), jax.ShapeDtypeStruct((N_SEL, t), F32)],
        compiler_params=pltpu.CompilerParams(dimension_semantics=("arbitrary", "arbitrary"),
                                             vmem_limit_bytes=VMEM_LIMIT),
        name="route",
    )(h1b, wq, keys)


def _gather_matrix(idx_ref, tokn, tab_ref):
    tiles = []
    for j in range(0, N_SEL, 2):
        lo = tab_ref[pl.ds(pl.multiple_of(idx_ref[tokn * N_SEL + j], ROW_WORDS), ROW_WORDS), :]
        hi = tab_ref[pl.ds(pl.multiple_of(idx_ref[tokn * N_SEL + j + 1], ROW_WORDS), ROW_WORDS), :]
        tiles.append(pltpu.bitcast(jnp.concatenate([lo, hi], axis=0), BF16))
    return jnp.concatenate(tiles, axis=0)


def _chunk_diag():
    row = lax.broadcasted_iota(jnp.int32, (TOKEN_CHUNKS, TOKEN_CHUNKS * N_SEL), 0)
    lane = lax.broadcasted_iota(jnp.int32, (TOKEN_CHUNKS, TOKEN_CHUNKS * N_SEL), 1)
    return row, (lane & (TOKEN_CHUNKS - 1)) == row


def _idx_copy(idx_hbm, half_block, dst, sem):
    n = dst.shape[0]
    return pltpu.make_async_copy(idx_hbm.at[pl.ds(half_block * n, n)], dst, sem)


def _for_each_half(idx_hbm, smem_a, smem_b, sems, process):
    step = pl.program_id(0)
    half = smem_a.shape[0] // N_SEL

    @pl.when(step == 0)
    def _():
        _idx_copy(idx_hbm, 0, smem_a, sems.at[0]).start()

    _idx_copy(idx_hbm, 2 * step + 1, smem_b, sems.at[1]).start()
    _idx_copy(idx_hbm, 2 * step, smem_a, sems.at[0]).wait()
    process(smem_a, 0)

    @pl.when(step + 1 < pl.num_programs(0))
    def _():
        _idx_copy(idx_hbm, 2 * step + 2, smem_a, sems.at[0]).start()

    _idx_copy(idx_hbm, 2 * step + 1, smem_b, sems.at[1]).wait()
    process(smem_b, half)


def _act_body(idx_hbm, x_ref, gate_ref, tab_ref, coef_ref, smem_a, smem_b, sems, zbuf):
    row, diag = _chunk_diag()
    nt = (((1,), (1,)), ((), ()))

    def half(idx_ref, base):
        for r0 in range(0, idx_ref.shape[0] // N_SEL, PEER_GROUP):
            zt = jnp.zeros((PEER_GROUP, TOKEN_CHUNKS * N_SEL), F32)
            for i in range(PEER_GROUP):
                w = _gather_matrix(idx_ref, r0 + i, tab_ref)
                y = lax.dot_general(x_ref[base + r0 + i].astype(BF16), w, nt, preferred_element_type=F32)
                zrow = jnp.sum(jnp.where(diag, y, 0.0), axis=0, keepdims=True)
                zt = jnp.where(row == i, zrow, zt)
            zbuf[base + r0:base + r0 + PEER_GROUP, :] = zt

    _for_each_half(idx_hbm, smem_a, smem_b, sems, half)

    rr = lax.broadcasted_iota(jnp.int32, (TOKEN_CHUNKS * N_SEL, N_SEL), 0)
    cc = lax.broadcasted_iota(jnp.int32, (TOKEN_CHUNKS * N_SEL, N_SEL), 1)
    fold = ((rr >> 3) == cc).astype(F32)
    act = jnp.dot(zbuf[...], fold, precision=HIGHEST, preferred_element_type=F32)
    coef_ref[...] = gate_ref[...].T * _gelu(act)


def _peer_scratch(rows):
    half_words = (rows // 2) * N_SEL
    return [pltpu.SMEM((half_words,), jnp.int32), pltpu.SMEM((half_words,), jnp.int32),
            pltpu.SemaphoreType.DMA((2,))]


def _act(idx_flat, h1, gate, utab):
    t = gate.shape[1]
    rows = min(PEER_ROWS, t)
    return pl.pallas_call(
        _act_body,
        grid=(t // rows,),
        in_specs=[pl.BlockSpec(memory_space=pl.ANY),
                  pl.BlockSpec((rows, TOKEN_CHUNKS, 128), lambda i: (i, 0, 0)),
                  pl.BlockSpec((N_SEL, rows), lambda i: (0, i)),
                  pl.BlockSpec(utab.shape, lambda i: (0, 0), pipeline_mode=pl.Buffered(1))],
        out_specs=pl.BlockSpec((rows, N_SEL), lambda i: (i, 0)),
        out_shape=jax.ShapeDtypeStruct((t, N_SEL), F32),
        scratch_shapes=_peer_scratch(rows) + [pltpu.VMEM((rows, TOKEN_CHUNKS * N_SEL), F32)],
        compiler_params=pltpu.CompilerParams(dimension_semantics=("arbitrary",), vmem_limit_bytes=VMEM_LIMIT),
        name="peer_act",
    )(idx_flat, h1.reshape(t, TOKEN_CHUNKS, 128), gate, utab)


def _combine_body(idx_hbm, coef_ref, tab_ref, ff_ref, smem_a, smem_b, sems, crep):
    row, diag = _chunk_diag()
    rr = lax.broadcasted_iota(jnp.int32, (N_SEL, TOKEN_CHUNKS * N_SEL), 0)
    cc = lax.broadcasted_iota(jnp.int32, (N_SEL, TOKEN_CHUNKS * N_SEL), 1)
    spread = ((cc >> 3) == rr).astype(BF16)
    crep[...] = jnp.dot(coef_ref[...].astype(BF16), spread, preferred_element_type=F32)

    def half(idx_ref, base):
        for r0 in range(0, idx_ref.shape[0] // N_SEL, PEER_GROUP):
            cg = crep[base + r0:base + r0 + PEER_GROUP, :]
            for i in range(PEER_GROUP):
                w = _gather_matrix(idx_ref, r0 + i, tab_ref)
                ci = jnp.broadcast_to(cg[i:i + 1, :], diag.shape)
                lhs = jnp.where(diag, ci, 0.0).astype(BF16)
                ff_ref[base + r0 + i] = jnp.dot(lhs, w, preferred_element_type=F32)

    _for_each_half(idx_hbm, smem_a, smem_b, sems, half)


def _combine(idx_flat, coef, vtab):
    t = coef.shape[0]
    rows = min(PEER_ROWS, t)
    ff = pl.pallas_call(
        _combine_body,
        grid=(t // rows,),
        in_specs=[pl.BlockSpec(memory_space=pl.ANY),
                  pl.BlockSpec((rows, N_SEL), lambda i: (i, 0)),
                  pl.BlockSpec(vtab.shape, lambda i: (0, 0), pipeline_mode=pl.Buffered(1))],
        out_specs=pl.BlockSpec((rows, TOKEN_CHUNKS, 128), lambda i: (i, 0, 0)),
        out_shape=jax.ShapeDtypeStruct((t, TOKEN_CHUNKS, 128), F32),
        scratch_shapes=_peer_scratch(rows) + [pltpu.VMEM((rows, TOKEN_CHUNKS * N_SEL), F32)],
        compiler_params=pltpu.CompilerParams(dimension_semantics=("arbitrary",), vmem_limit_bytes=VMEM_LIMIT),
        name="peer_combine",
    )(idx_flat, coef, vtab)
    return ff.reshape(t, D_MODEL)


def _final_body(h1_ref, ff_ref, p_ref, wg, bg, wp, ln2g, ln2b, out_ref):
    h1 = h1_ref[...]
    gate = _sigmoid(jnp.dot(h1.astype(BF16), wg[...], preferred_element_type=F32) + bg[...])
    ple = gate * jnp.dot(p_ref[...].astype(BF16), wp[...], preferred_element_type=F32)
    out_ref[...] = _layer_norm(ALPHA * h1 + ff_ref[...] + ple, ln2g[...], ln2b[...])


def _final(h1, ff, p2d, wg, bg, wp, ln2g, ln2b):
    t = h1.shape[0]
    rows = min(FINAL_ROWS, t)
    const = lambda a: pl.BlockSpec(a.shape, lambda i: (0,) * a.ndim)
    return pl.pallas_call(
        _final_body,
        grid=(t // rows,),
        in_specs=[pl.BlockSpec((rows, D_MODEL), lambda i: (i, 0)),
                  pl.BlockSpec((rows, D_MODEL), lambda i: (i, 0)),
                  pl.BlockSpec((rows, PLE_DIM), lambda i: (i, 0)),
                  const(wg), const(bg), const(wp), const(ln2g), const(ln2b)],
        out_specs=pl.BlockSpec((rows, D_MODEL), lambda i: (i, 0)),
        out_shape=jax.ShapeDtypeStruct((t, D_MODEL), F32),
        compiler_params=pltpu.CompilerParams(dimension_semantics=("arbitrary",), vmem_limit_bytes=VMEM_LIMIT),
        name="final",
    )(h1, ff, p2d, wg, bg, wp, ln2g, ln2b)


def _pack_body(tab_ref, out_ref):
    out_ref[...] = pltpu.bitcast(tab_ref[...].astype(BF16), jnp.int32)


def _pack_table(tab):
    view_rows = tab.size // 128
    rows = min(PACK_ROWS, view_rows)
    return pl.pallas_call(
        _pack_body,
        grid=(view_rows // rows,),
        in_specs=[pl.BlockSpec((rows, 128), lambda i: (i, 0))],
        out_specs=pl.BlockSpec((rows // 2, 128), lambda i: (i, 0)),
        out_shape=jax.ShapeDtypeStruct((view_rows // 2, 128), jnp.int32),
        compiler_params=pltpu.CompilerParams(dimension_semantics=("arbitrary",), vmem_limit_bytes=VMEM_LIMIT),
        name="pack_table",
    )(tab.reshape(view_rows, 128))


def kernel(x, p, ln0_g, ln0_b, w_in, b_in, conv_w, conv_b, gn_g, gn_b, sg_ln_g, sg_ln_b, sg_w, sg_b, w_o, b_o,
           ln1_g, ln1_b, peer_wq, peer_keys, peer_u, peer_v, ple_wp, ple_wg, ple_bg, ln2_g, ln2_b):
    bsz, seq, d = x.shape
    assert w_in.shape[0] == 1, "the input norm is fused into the single layer's mixer"
    t = bsz * seq
    row = lambda a: a.reshape(1, -1)
    avg = jnp.asarray(np.kron(np.eye(CONV_WIDTH // GROUP), np.full((GROUP, GROUP), 1.0 / GROUP)), BF16)
    wcat = sg_w[0].transpose(1, 0, 2).reshape(SG_CHUNK, N_SG_HEADS * SG_CHUNK)
    sgbias = jnp.repeat(sg_b[0].T, GROUP, axis=1)

    h1, h1b = _mixer(x.reshape(t, d), seq, row(ln0_g), row(ln0_b), w_in[0].astype(BF16), row(b_in[0]), conv_w[0],
                     row(conv_b[0]), row(gn_g[0]), row(gn_b[0]), row(sg_ln_g[0]), row(sg_ln_b[0]), wcat, sgbias, avg,
                     w_o[0, :CONV_WIDTH].astype(BF16), w_o[0, CONV_WIDTH:].astype(BF16), row(b_o[0]),
                     row(ln1_g[0]), row(ln1_b[0]))
    idx, gate = _route(h1b, peer_wq[0].astype(BF16), peer_keys[0].astype(BF16))
    idx_flat = idx.T.reshape(-1)
    coef = _act(idx_flat, h1, gate, _pack_table(peer_u))
    ff = _combine(idx_flat, coef, _pack_table(peer_v))
    out = _final(h1, ff, p.reshape(t, PLE_DIM), ple_wg[0].astype(BF16), row(ple_bg[0]), ple_wp[0].astype(BF16),
                 row(ln2_g[0]), row(ln2_b[0]))
    return out.reshape(bsz, seq, d)
```
